```python
import math
import jax, jax.numpy as jnp
from jax import lax
import numpy as np

D_MODEL = 1024
BATCH = 16
SEQ = 2048
DEPTH = 4

CHUNK = 64
D_MIX = D_MODEL
LRU_HEADS = 8
LRU_WIDTH = D_MIX // 2
LRU_HEAD_DIM = LRU_WIDTH // LRU_HEADS
SC_GROUPS = 8
SC_WIDTH = D_MIX - LRU_WIDTH
SC_GROUP_DIM = SC_WIDTH // SC_GROUPS
LRU_CONV = 4
SC_CONV = 3
MLP_CONV = 3
D_FF = 3 * D_MODEL
LRU_C = 8.0
IN_COLS = 2 * LRU_WIDTH + 3 * SC_WIDTH
EPS = 1e-6

kernel_name = "hybrid_rglru_shortconv_convffn_trunk"


def rms_norm(x, g):
    xf = x.astype(jnp.float32)
    y = xf * lax.rsqrt(jnp.mean(xf * xf, axis=-1, keepdims=True) + EPS)
    return (y * g.astype(jnp.float32)).astype(x.dtype)


def causal_dwconv(x, w, b=None):
    k_width = w.shape[0]
    s = x.shape[1]
    xp = jnp.pad(x, ((0, 0), (k_width - 1, 0), (0, 0)))
    y = xp[:, 0:s] * w[0]
    for k in range(1, k_width):
        y = y + xp[:, k:k + s] * w[k]
    if b is not None:
        y = y + b
    return y


def block_diag_linear(x, w, b):
    bsz, s, _ = x.shape
    h, dh, _ = w.shape
    xh = x.reshape(bsz, s, h, dh)
    y = jnp.einsum('bshd,hde->bshe', xh, w) + b
    return y.reshape(bsz, s, h * dh)


def linear_scan(a, b):
    def combine(left, right):
        a_l, b_l = left
        a_r, b_r = right
        return a_l * a_r, a_r * b_l + b_r
    _, h = lax.associative_scan(combine, (a, b), axis=1)
    return h


def rg_lru(x, w_a, b_a, w_x, b_x, lam):
    xf = x.astype(jnp.float32)
    r = jax.nn.sigmoid(block_diag_linear(xf, w_a.astype(jnp.float32), b_a.astype(jnp.float32)))
    i = jax.nn.sigmoid(block_diag_linear(xf, w_x.astype(jnp.float32), b_x.astype(jnp.float32)))
    log_a = -LRU_C * r * jax.nn.softplus(-lam.astype(jnp.float32))
    a = jnp.exp(log_a)
    mult = jnp.sqrt(jnp.clip(-jnp.expm1(2.0 * log_a), 0.0, None))
    h = linear_scan(a, mult * (i * xf))
    return h.astype(x.dtype)


def setup_inputs(seed: int = 0) -> dict:
    key = jax.random.key(seed)
    ks = jax.random.split(key, 24)
    f32 = jnp.float32

    def nrm(k, shape, fan_in):
        return jax.random.normal(k, shape, f32) * (fan_in ** -0.5)

    def gain(k, shape):
        return 1.0 + 0.05 * jax.random.normal(k, shape, f32)

    def bias(k, shape):
        return 0.01 * jax.random.normal(k, shape, f32)

    u = jax.random.uniform(ks[7], (DEPTH, LRU_WIDTH), f32, 0.9, 0.999)
    s = u ** (1.0 / LRU_C)
    lam = jnp.log(s) - jnp.log1p(-s)

    return {
        "x": jax.random.normal(ks[0], (BATCH, SEQ, D_MODEL), f32),
        "norm1_g": gain(ks[1], (DEPTH, D_MODEL)),
        "w_in": nrm(ks[2], (DEPTH, D_MODEL, IN_COLS), D_MODEL),
        "lru_conv_w": nrm(ks[3], (DEPTH, LRU_CONV, LRU_WIDTH), LRU_CONV),
        "lru_conv_b": bias(ks[4], (DEPTH, LRU_WIDTH)),
        "lru_w_a": nrm(ks[5], (DEPTH, LRU_HEADS, LRU_HEAD_DIM, LRU_HEAD_DIM), LRU_HEAD_DIM),
        "lru_b_a": bias(ks[6], (DEPTH, LRU_HEADS, LRU_HEAD_DIM)),
        "lru_w_x": nrm(ks[8], (DEPTH, LRU_HEADS, LRU_HEAD_DIM, LRU_HEAD_DIM), LRU_HEAD_DIM),
        "lru_b_x": bias(ks[9], (DEPTH, LRU_HEADS, LRU_HEAD_DIM)),
        "lru_lambda": lam,
        "sc_conv_w": nrm(ks[10], (DEPTH, SC_CONV, SC_WIDTH), SC_CONV),
        "lru_out_g": gain(ks[11], (DEPTH, LRU_WIDTH)),
        "sc_out_g": gain(ks[12], (DEPTH, SC_WIDTH)),
        "w_out": nrm(ks[13], (DEPTH, D_MIX, D_MODEL), D_MIX),
        "norm2_g": gain(ks[14], (DEPTH, D_MODEL)),
        "mlp_w_up": nrm(ks[15], (DEPTH, D_MODEL, D_FF), D_MODEL),
        "mlp_w_gate": nrm(ks[16], (DEPTH, D_MODEL, D_FF), D_MODEL),
        "mlp_conv_w": nrm(ks[17], (DEPTH, MLP_CONV, D_FF), MLP_CONV),
        "mlp_conv_b": bias(ks[18], (DEPTH, D_FF)),
        "mlp_w_down": nrm(ks[19], (DEPTH, D_FF, D_MODEL), D_FF),
        "final_g": gain(ks[20], (D_MODEL,)),
    }


def reference(x, norm1_g, w_in, lru_conv_w, lru_conv_b, lru_w_a, lru_b_a, lru_w_x, lru_b_x,
              lru_lambda, sc_conv_w, lru_out_g, sc_out_g, w_out, norm2_g, mlp_w_up, mlp_w_gate,
              mlp_conv_w, mlp_conv_b, mlp_w_down, final_g):
    h = x
    for l in range(DEPTH):
        u = rms_norm(h, norm1_g[l])
        proj = jnp.einsum('bsd,dc->bsc', u, w_in[l])
        o = 0
        lru_x = proj[..., o:o + LRU_WIDTH]; o += LRU_WIDTH
        lru_gate = proj[..., o:o + LRU_WIDTH]; o += LRU_WIDTH
        sc_b = proj[..., o:o + SC_WIDTH]; o += SC_WIDTH
        sc_c = proj[..., o:o + SC_WIDTH]; o += SC_WIDTH
        sc_v = proj[..., o:o + SC_WIDTH]

        xa = causal_dwconv(lru_x, lru_conv_w[l], lru_conv_b[l])
        ya = rg_lru(xa, lru_w_a[l], lru_b_a[l], lru_w_x[l], lru_b_x[l], lru_lambda[l])
        ya = ya * jax.nn.gelu(lru_gate, approximate=True)

        yb = sc_b * causal_dwconv(sc_c * sc_v, sc_conv_w[l])

        y = jnp.concatenate([rms_norm(ya, lru_out_g[l]), rms_norm(yb, sc_out_g[l])], axis=-1)
        h = h + jnp.einsum('bsc,cd->bsd', y, w_out[l])

        v = rms_norm(h, norm2_g[l])
        up = jnp.einsum('bsd,df->bsf', v, mlp_w_up[l])
        gate = jnp.einsum('bsd,df->bsf', v, mlp_w_gate[l])
        act = jax.nn.gelu(causal_dwconv(up, mlp_conv_w[l], mlp_conv_b[l]), approximate=True) * gate
        h = h + jnp.einsum('bsf,fd->bsd', act, mlp_w_down[l])
    return rms_norm(h, final_g)
```

```python
import functools
import math

import jax
import jax.numpy as jnp
from jax import lax
from jax.experimental import pallas as pl
from jax.experimental.pallas import tpu as pltpu

D_MODEL = 1024
LRU_WIDTH = 512
SC_WIDTH = 512
LRU_HEADS = 8
LRU_HEAD_DIM = LRU_WIDTH // LRU_HEADS
LRU_CONV = 4
SC_CONV = 3
MLP_CONV = 3
D_FF = 3 * D_MODEL
LRU_C = 8.0
IN_COLS = 2 * LRU_WIDTH + 3 * SC_WIDTH
EPS = 1e-6

T_BLK = 32
F_BLK = 512
GATE_BLK = 256
VMEM_LIMIT_BYTES = 56 * 1024 * 1024

_GELU_C1 = math.sqrt(2.0 / math.pi)
_GELU_C2 = _GELU_C1 * 0.044715


def _rms_norm(x, g):
    ms = jnp.mean(x * x, axis=-1, keepdims=True)
    return x * lax.rsqrt(ms + EPS) * g


def _gelu_tanh(x):
    inner = x * (_GELU_C1 + _GELU_C2 * (x * x))
    hx = 0.5 * x
    return hx + hx * jnp.tanh(inner)


def _bdot(a, w):
    return jnp.dot(a.astype(jnp.bfloat16), w, preferred_element_type=jnp.float32)


def _mixer_kernel(h_ref, g1_ref, win_ref, cw_ref, cb_ref, wg_ref, ba_ref, bx_ref, lam_ref,
                  scw_ref, glru_ref, gsc_ref, wout_ref, o_ref,
                  xbuf, cvbuf, abuf, ybuf, hstate, *, nb, tb):
    m = nb * tb
    hx_rows = (LRU_CONV - 1) * nb
    hc_rows = (SC_CONV - 1) * nb

    @pl.when(pl.program_id(0) == 0)
    def _():
        xbuf[0:hx_rows, :] = jnp.zeros((hx_rows, LRU_WIDTH), jnp.float32)
        cvbuf[0:hc_rows, :] = jnp.zeros((hc_rows, SC_WIDTH), jnp.float32)
        hstate[...] = jnp.zeros_like(hstate)

    h = h_ref[...]
    u = _rms_norm(h, g1_ref[...]).astype(jnp.bfloat16)

    xbuf[hx_rows:hx_rows + m, :] = jnp.dot(u, win_ref[:, 0:LRU_WIDTH],
                                           preferred_element_type=jnp.float32)
    xa = cb_ref[...]
    for j in range(LRU_CONV):
        off = j * nb
        xa = xa + cw_ref[j:j + 1, :] * xbuf[off:off + m, :]
    xbuf[0:hx_rows, :] = xbuf[m:m + hx_rows, :]

    xa_b = xa.astype(jnp.bfloat16)
    r_parts, i_parts = [], []
    for c in range(LRU_WIDTH // GATE_BLK):
        g = jnp.dot(xa_b[:, c * GATE_BLK:(c + 1) * GATE_BLK], wg_ref[c],
                    preferred_element_type=jnp.float32)
        r_parts.append(g[:, 0:GATE_BLK])
        i_parts.append(g[:, GATE_BLK:2 * GATE_BLK])
    r = jax.nn.sigmoid(jnp.concatenate(r_parts, axis=-1) + ba_ref[...])
    ig = jax.nn.sigmoid(jnp.concatenate(i_parts, axis=-1) + bx_ref[...])
    log_a = (-LRU_C) * r * jax.nn.softplus(-lam_ref[...])
    a = jnp.exp(log_a)
    mult = jnp.sqrt(jnp.maximum(-jnp.tanh(log_a) * (1.0 + a * a), 0.0))
    abuf[...] = a
    ybuf[...] = mult * (ig * xa)

    hs = hstate[...]
    for t in range(tb):
        rows = slice(t * nb, (t + 1) * nb)
        hs = abuf[rows, :] * hs + ybuf[rows, :]
        ybuf[rows, :] = hs
    hstate[...] = hs

    lru_gate = jnp.dot(u, win_ref[:, LRU_WIDTH:2 * LRU_WIDTH], preferred_element_type=jnp.float32)
    ya = ybuf[...] * _gelu_tanh(lru_gate)

    o = 2 * LRU_WIDTH
    sc_b = jnp.dot(u, win_ref[:, o:o + SC_WIDTH], preferred_element_type=jnp.float32)
    sc_c = jnp.dot(u, win_ref[:, o + SC_WIDTH:o + 2 * SC_WIDTH], preferred_element_type=jnp.float32)
    sc_v = jnp.dot(u, win_ref[:, o + 2 * SC_WIDTH:o + 3 * SC_WIDTH], preferred_element_type=jnp.float32)
    cvbuf[hc_rows:hc_rows + m, :] = sc_c * sc_v
    cv = scw_ref[0:1, :] * cvbuf[0:m, :]
    for j in range(1, SC_CONV):
        off = j * nb
        cv = cv + scw_ref[j:j + 1, :] * cvbuf[off:off + m, :]
    cvbuf[0:hc_rows, :] = cvbuf[m:m + hc_rows, :]
    yb = sc_b * cv

    ya_n = _rms_norm(ya, glru_ref[...])
    yb_n = _rms_norm(yb, gsc_ref[...])
    o_ref[...] = (h + _bdot(ya_n, wout_ref[0:LRU_WIDTH, :])
                  + _bdot(yb_n, wout_ref[LRU_WIDTH:LRU_WIDTH + SC_WIDTH, :]))


def _mlp_kernel(h_ref, g2_ref, wup_ref, wgate_ref, cw_ref, cb_ref, wdown_ref, gfin_ref, o_ref,
                ubuf, *, nb, tb, final_norm):
    m = nb * tb
    hu_rows = (MLP_CONV - 1) * nb

    @pl.when(pl.program_id(0) == 0)
    def _():
        ubuf[0:hu_rows, :] = jnp.zeros((hu_rows, D_FF), jnp.float32)

    h = h_ref[...]
    v = _rms_norm(h, g2_ref[...]).astype(jnp.bfloat16)
    acc = h
    for k in range(D_FF // F_BLK):
        cols = slice(k * F_BLK, (k + 1) * F_BLK)
        ubuf[hu_rows:hu_rows + m, cols] = jnp.dot(v, wup_ref[:, cols],
                                                  preferred_element_type=jnp.float32)
        gate = jnp.dot(v, wgate_ref[:, cols], preferred_element_type=jnp.float32)
        c = cb_ref[:, cols]
        for j in range(MLP_CONV):
            off = j * nb
            c = c + cw_ref[j:j + 1, cols] * ubuf[off:off + m, cols]
        ubuf[0:hu_rows, cols] = ubuf[m:m + hu_rows, cols]
        act = _gelu_tanh(c) * gate
        acc = acc + _bdot(act, wdown_ref[cols, :])
    if final_norm:
        acc = _rms_norm(acc, gfin_ref[...])
    o_ref[...] = acc


def _layer_spec(shape):
    nd = len(shape)
    return lambda l: pl.BlockSpec((None,) + tuple(shape), lambda i: (l,) + (0,) * nd,
                                  pipeline_mode=pl.Buffered(1))


def _mixer_call(h, l, p, nb, tb):
    n_rows = h.shape[0]
    m = nb * tb
    row_spec = pl.BlockSpec((m, D_MODEL), lambda i: (i, 0))
    params = [
        (p["norm1_g"], (1, D_MODEL)),
        (p["w_in"], (D_MODEL, IN_COLS)),
        (p["lru_conv_w"], (LRU_CONV, LRU_WIDTH)),
        (p["lru_conv_b"], (1, LRU_WIDTH)),
        (p["w_gates"], (LRU_WIDTH // GATE_BLK, GATE_BLK, 2 * GATE_BLK)),
        (p["lru_b_a"], (1, LRU_WIDTH)),
        (p["lru_b_x"], (1, LRU_WIDTH)),
        (p["lru_lambda"], (1, LRU_WIDTH)),
        (p["sc_conv_w"], (SC_CONV, SC_WIDTH)),
        (p["lru_out_g"], (1, LRU_WIDTH)),
        (p["sc_out_g"], (1, SC_WIDTH)),
        (p["w_out"], (D_MODEL, D_MODEL)),
    ]
    return pl.pallas_call(
        functools.partial(_mixer_kernel, nb=nb, tb=tb),
        out_shape=jax.ShapeDtypeStruct(h.shape, h.dtype),
        grid=(n_rows // m,),
        in_specs=[row_spec] + [_layer_spec(s)(l) for _, s in params],
        out_specs=row_spec,
        scratch_shapes=[
            pltpu.VMEM(((LRU_CONV - 1) * nb + m, LRU_WIDTH), jnp.float32),
            pltpu.VMEM(((SC_CONV - 1) * nb + m, SC_WIDTH), jnp.float32),
            pltpu.VMEM((m, LRU_WIDTH), jnp.float32),
            pltpu.VMEM((m, LRU_WIDTH), jnp.float32),
            pltpu.VMEM((nb, LRU_WIDTH), jnp.float32),
        ],
        compiler_params=pltpu.CompilerParams(dimension_semantics=("arbitrary",),
                                             vmem_limit_bytes=VMEM_LIMIT_BYTES),
        name=f"mixer_l{l}",
    )(h, *[a for a, _ in params])


def _mlp_call(h, l, p, nb, tb, final_norm):
    n_rows = h.shape[0]
    m = nb * tb
    row_spec = pl.BlockSpec((m, D_MODEL), lambda i: (i, 0))
    params = [
        (p["norm2_g"], (1, D_MODEL)),
        (p["mlp_w_up"], (D_MODEL, D_FF)),
        (p["mlp_w_gate"], (D_MODEL, D_FF)),
        (p["mlp_conv_w"], (MLP_CONV, D_FF)),
        (p["mlp_conv_b"], (1, D_FF)),
        (p["mlp_w_down"], (D_FF, D_MODEL)),
    ]
    fin_spec = pl.BlockSpec((1, D_MODEL), lambda i: (0, 0), pipeline_mode=pl.Buffered(1))
    return pl.pallas_call(
        functools.partial(_mlp_kernel, nb=nb, tb=tb, final_norm=final_norm),
        out_shape=jax.ShapeDtypeStruct(h.shape, h.dtype),
        grid=(n_rows // m,),
        in_specs=[row_spec] + [_layer_spec(s)(l) for _, s in params] + [fin_spec],
        out_specs=row_spec,
        scratch_shapes=[pltpu.VMEM(((MLP_CONV - 1) * nb + m, D_FF), jnp.float32)],
        compiler_params=pltpu.CompilerParams(dimension_semantics=("arbitrary",),
                                             vmem_limit_bytes=VMEM_LIMIT_BYTES),
        name=f"mlp_l{l}",
    )(h, *[a for a, _ in params], p["final_g"])


def _block_diag_gates(w_a, w_x):
    depth = w_a.shape[0]
    hpg = GATE_BLK // LRU_HEAD_DIM
    n_grp = LRU_HEADS // hpg
    eye = jnp.eye(hpg, dtype=w_a.dtype)

    def bd(w):
        w = w.reshape(depth, n_grp, hpg, LRU_HEAD_DIM, LRU_HEAD_DIM)
        full = jnp.einsum('lghde,hk->lghdke', w, eye)
        return full.reshape(depth, n_grp, GATE_BLK, GATE_BLK)

    return jnp.concatenate([bd(w_a), bd(w_x)], axis=-1)


def kernel(x, norm1_g, w_in, lru_conv_w, lru_conv_b, lru_w_a, lru_b_a, lru_w_x, lru_b_x, lru_lambda, sc_conv_w, lru_out_g, sc_out_g, w_out, norm2_g, mlp_w_up, mlp_w_gate, mlp_conv_w, mlp_conv_b, mlp_w_down, final_g):
    bsz, seq, d = x.shape
    depth = w_in.shape[0]
    bf = jnp.bfloat16
    row = lambda a: a.reshape(depth, 1, -1)
    p = {
        "norm1_g": row(norm1_g), "w_in": w_in.astype(bf),
        "lru_conv_w": lru_conv_w, "lru_conv_b": row(lru_conv_b),
        "w_gates": _block_diag_gates(lru_w_a, lru_w_x).astype(bf),
        "lru_b_a": row(lru_b_a), "lru_b_x": row(lru_b_x), "lru_lambda": row(lru_lambda),
        "sc_conv_w": sc_conv_w, "lru_out_g": row(lru_out_g), "sc_out_g": row(sc_out_g),
        "w_out": w_out.astype(bf), "norm2_g": row(norm2_g),
        "mlp_w_up": mlp_w_up.astype(bf), "mlp_w_gate": mlp_w_gate.astype(bf),
        "mlp_conv_w": mlp_conv_w, "mlp_conv_b": row(mlp_conv_b),
        "mlp_w_down": mlp_w_down.astype(bf), "final_g": final_g.reshape(1, -1),
    }
    h = jnp.transpose(x, (1, 0, 2)).reshape(seq * bsz, d)
    for l in range(depth):
        h = _mixer_call(h, l, p, bsz, T_BLK)
        h = _mlp_call(h, l, p, bsz, T_BLK, final_norm=(l == depth - 1))
    return jnp.transpose(h.reshape(seq, bsz, d), (1, 0, 2))
```

```python
import functools
import math

import jax
import jax.numpy as jnp
from jax import lax
from jax.experimental import pallas as pl
from jax.experimental.pallas import tpu as pltpu

D_MODEL = 1024
LRU_WIDTH = 512
SC_WIDTH = 512
LRU_HEADS = 8
LRU_HEAD_DIM = LRU_WIDTH // LRU_HEADS
LRU_CONV = 4
SC_CONV = 3
MLP_CONV = 3
D_FF = 3 * D_MODEL
LRU_C = 8.0
IN_COLS = 2 * LRU_WIDTH + 3 * SC_WIDTH
EPS = 1e-6

T_BLK = 64
F_BLK = 512
ROW_GROUPS = 4
GATE_BLK = 256
VMEM_LIMIT_BYTES = 56 * 1024 * 1024

_GELU_C1 = math.sqrt(2.0 / math.pi)
_GELU_C2 = _GELU_C1 * 0.044715


def _rms_norm(x, g):
    ms = jnp.mean(x * x, axis=-1, keepdims=True)
    return x * lax.rsqrt(ms + EPS) * g


def _gelu_tanh(x):
    inner = x * (_GELU_C1 + _GELU_C2 * (x * x))
    hx = 0.5 * x
    return hx + hx * jnp.tanh(inner)


def _bdot(a, w):
    return jnp.dot(a.astype(jnp.bfloat16), w, preferred_element_type=jnp.float32)


def _mixer_kernel(h_ref, g1_ref, win_ref, cw_ref, cb_ref, wg_ref, ba_ref, bx_ref, lam_ref,
                  scw_ref, glru_ref, gsc_ref, wout_ref, o_ref,
                  xbuf, cvbuf, abuf, ybuf, hstate, *, nb, tb):
    m = nb * tb
    hx_rows = (LRU_CONV - 1) * nb
    hc_rows = (SC_CONV - 1) * nb

    @pl.when(pl.program_id(0) == 0)
    def _():
        xbuf[0:hx_rows, :] = jnp.zeros((hx_rows, LRU_WIDTH), jnp.float32)
        cvbuf[0:hc_rows, :] = jnp.zeros((hc_rows, SC_WIDTH), jnp.float32)
        hstate[...] = jnp.zeros_like(hstate)

    h = h_ref[...]
    u = _rms_norm(h, g1_ref[...]).astype(jnp.bfloat16)

    xbuf[hx_rows:hx_rows + m, :] = jnp.dot(u, win_ref[:, 0:LRU_WIDTH],
                                           preferred_element_type=jnp.float32)
    xa = cb_ref[...]
    for j in range(LRU_CONV):
        off = j * nb
        xa = xa + cw_ref[j:j + 1, :] * xbuf[off:off + m, :]
    xbuf[0:hx_rows, :] = xbuf[m:m + hx_rows, :]

    xa_b = xa.astype(jnp.bfloat16)
    r_parts, i_parts = [], []
    for c in range(LRU_WIDTH // GATE_BLK):
        g = jnp.dot(xa_b[:, c * GATE_BLK:(c + 1) * GATE_BLK], wg_ref[c],
                    preferred_element_type=jnp.float32)
        r_parts.append(g[:, 0:GATE_BLK])
        i_parts.append(g[:, GATE_BLK:2 * GATE_BLK])
    r = jax.nn.sigmoid(jnp.concatenate(r_parts, axis=-1) + ba_ref[...])
    ig = jax.nn.sigmoid(jnp.concatenate(i_parts, axis=-1) + bx_ref[...])
    log_a = (-LRU_C) * r * jax.nn.softplus(-lam_ref[...])
    a = jnp.exp(log_a)
    mult = jnp.sqrt(jnp.maximum(-jnp.tanh(log_a) * (1.0 + a * a), 0.0))
    abuf[...] = a
    ybuf[...] = mult * (ig * xa)

    hs = hstate[...]
    for t in range(tb):
        rows = slice(t * nb, (t + 1) * nb)
        hs = abuf[rows, :] * hs + ybuf[rows, :]
        ybuf[rows, :] = hs
    hstate[...] = hs

    lru_gate = jnp.dot(u, win_ref[:, LRU_WIDTH:2 * LRU_WIDTH], preferred_element_type=jnp.float32)
    ya = ybuf[...] * _gelu_tanh(lru_gate)

    o = 2 * LRU_WIDTH
    sc_b = jnp.dot(u, win_ref[:, o:o + SC_WIDTH], preferred_element_type=jnp.float32)
    sc_c = jnp.dot(u, win_ref[:, o + SC_WIDTH:o + 2 * SC_WIDTH], preferred_element_type=jnp.float32)
    sc_v = jnp.dot(u, win_ref[:, o + 2 * SC_WIDTH:o + 3 * SC_WIDTH], preferred_element_type=jnp.float32)
    cvbuf[hc_rows:hc_rows + m, :] = sc_c * sc_v
    cv = scw_ref[0:1, :] * cvbuf[0:m, :]
    for j in range(1, SC_CONV):
        off = j * nb
        cv = cv + scw_ref[j:j + 1, :] * cvbuf[off:off + m, :]
    cvbuf[0:hc_rows, :] = cvbuf[m:m + hc_rows, :]
    yb = sc_b * cv

    ya_n = _rms_norm(ya, glru_ref[...])
    yb_n = _rms_norm(yb, gsc_ref[...])
    o_ref[...] = (h + _bdot(ya_n, wout_ref[0:LRU_WIDTH, :])
                  + _bdot(yb_n, wout_ref[LRU_WIDTH:LRU_WIDTH + SC_WIDTH, :]))


def _causal_taps(prev, cur, w_ref, cols, nb):
    k_width = w_ref.shape[0]
    m = cur.shape[0]
    ext = jnp.concatenate([prev, cur], axis=0)
    out = w_ref[k_width - 1:k_width, cols] * cur
    for j in range(k_width - 1):
        out = out + w_ref[j:j + 1, cols] * ext[j * nb:j * nb + m, :]
    return out


def _mlp_kernel(h_ref, g2_ref, wup_ref, wgate_ref, cw_ref, cb_ref, wdown_ref, gfin_ref, o_ref,
                uhalo, ubuf, gbuf, *, nb, tb, final_norm):
    m = nb * tb
    hu_rows = (MLP_CONV - 1) * nb

    @pl.when(pl.program_id(0) == 0)
    def _():
        uhalo[...] = jnp.zeros_like(uhalo)

    n_f = D_FF // F_BLK
    mg = m // ROW_GROUPS
    v = [_rms_norm(h_ref[g * mg:(g + 1) * mg, :], g2_ref[...]).astype(jnp.bfloat16)
         for g in range(ROW_GROUPS)]

    def up_gate(g, k):
        cols = slice(k * F_BLK, (k + 1) * F_BLK)
        s = k % 2
        r0 = g * mg
        if g == 0:
            ubuf[s, 0:hu_rows, :] = uhalo[:, cols]
        ubuf[s, hu_rows + r0:hu_rows + r0 + mg, :] = jnp.dot(
            v[g], wup_ref[:, cols], preferred_element_type=jnp.float32)
        gbuf[s, r0:r0 + mg, :] = jnp.dot(v[g], wgate_ref[:, cols],
                                         preferred_element_type=jnp.float32)

    def act_down(g, k):
        cols = slice(k * F_BLK, (k + 1) * F_BLK)
        s = k % 2
        r0 = g * mg
        c = cb_ref[:, cols]
        for j in range(MLP_CONV):
            c = c + cw_ref[j:j + 1, cols] * ubuf[s, r0 + j * nb:r0 + j * nb + mg, :]
        if g == ROW_GROUPS - 1:
            uhalo[:, cols] = ubuf[s, m:m + hu_rows, :]
        act = _gelu_tanh(c) * gbuf[s, r0:r0 + mg, :]
        return _bdot(act, wdown_ref[cols, :])

    for g in range(ROW_GROUPS):
        up_gate(g, 0)
    acc = [h_ref[g * mg:(g + 1) * mg, :] for g in range(ROW_GROUPS)]
    for k in range(n_f):
        if k + 1 < n_f:
            for g in range(ROW_GROUPS):
                up_gate(g, k + 1)
        for g in range(ROW_GROUPS):
            acc[g] = acc[g] + act_down(g, k)
    for g in range(ROW_GROUPS):
        out = acc[g]
        if final_norm:
            out = _rms_norm(out, gfin_ref[...])
        o_ref[g * mg:(g + 1) * mg, :] = out


def _layer_spec(shape):
    nd = len(shape)
    return lambda l: pl.BlockSpec((None,) + tuple(shape), lambda i: (l,) + (0,) * nd,
                                  pipeline_mode=pl.Buffered(1))


def _mixer_call(h, l, p, nb, tb):
    n_rows = h.shape[0]
    m = nb * tb
    row_spec = pl.BlockSpec((m, D_MODEL), lambda i: (i, 0))
    params = [
        (p["norm1_g"], (1, D_MODEL)),
        (p["w_in"], (D_MODEL, IN_COLS)),
        (p["lru_conv_w"], (LRU_CONV, LRU_WIDTH)),
        (p["lru_conv_b"], (1, LRU_WIDTH)),
        (p["w_gates"], (LRU_WIDTH // GATE_BLK, GATE_BLK, 2 * GATE_BLK)),
        (p["lru_b_a"], (1, LRU_WIDTH)),
        (p["lru_b_x"], (1, LRU_WIDTH)),
        (p["lru_lambda"], (1, LRU_WIDTH)),
        (p["sc_conv_w"], (SC_CONV, SC_WIDTH)),
        (p["lru_out_g"], (1, LRU_WIDTH)),
        (p["sc_out_g"], (1, SC_WIDTH)),
        (p["w_out"], (D_MODEL, D_MODEL)),
    ]
    return pl.pallas_call(
        functools.partial(_mixer_kernel, nb=nb, tb=tb),
        out_shape=jax.ShapeDtypeStruct(h.shape, h.dtype),
        grid=(n_rows // m,),
        in_specs=[row_spec] + [_layer_spec(s)(l) for _, s in params],
        out_specs=row_spec,
        scratch_shapes=[
            pltpu.VMEM(((LRU_CONV - 1) * nb + m, LRU_WIDTH), jnp.float32),
            pltpu.VMEM(((SC_CONV - 1) * nb + m, SC_WIDTH), jnp.float32),
            pltpu.VMEM((m, LRU_WIDTH), jnp.float32),
            pltpu.VMEM((m, LRU_WIDTH), jnp.float32),
            pltpu.VMEM((nb, LRU_WIDTH), jnp.float32),
        ],
        compiler_params=pltpu.CompilerParams(dimension_semantics=("arbitrary",),
                                             vmem_limit_bytes=VMEM_LIMIT_BYTES),
        name=f"mixer_l{l}",
    )(h, *[a for a, _ in params])


def _mlp_call(h, l, p, nb, tb, final_norm):
    n_rows = h.shape[0]
    m = nb * tb
    row_spec = pl.BlockSpec((m, D_MODEL), lambda i: (i, 0))
    params = [
        (p["norm2_g"], (1, D_MODEL)),
        (p["mlp_w_up"], (D_MODEL, D_FF)),
        (p["mlp_w_gate"], (D_MODEL, D_FF)),
        (p["mlp_conv_w"], (MLP_CONV, D_FF)),
        (p["mlp_conv_b"], (1, D_FF)),
        (p["mlp_w_down"], (D_FF, D_MODEL)),
    ]
    fin_spec = pl.BlockSpec((1, D_MODEL), lambda i: (0, 0), pipeline_mode=pl.Buffered(1))
    return pl.pallas_call(
        functools.partial(_mlp_kernel, nb=nb, tb=tb, final_norm=final_norm),
        out_shape=jax.ShapeDtypeStruct(h.shape, h.dtype),
        grid=(n_rows // m,),
        in_specs=[row_spec] + [_layer_spec(s)(l) for _, s in params] + [fin_spec],
        out_specs=row_spec,
        scratch_shapes=[
            pltpu.VMEM(((MLP_CONV - 1) * nb, D_FF), jnp.float32),
            pltpu.VMEM((2, (MLP_CONV - 1) * nb + m, F_BLK), jnp.float32),
            pltpu.VMEM((2, m, F_BLK), jnp.float32),
        ],
        compiler_params=pltpu.CompilerParams(dimension_semantics=("arbitrary",),
                                             vmem_limit_bytes=VMEM_LIMIT_BYTES),
        name=f"mlp_l{l}",
    )(h, *[a for a, _ in params], p["final_g"])


def _block_diag_gates(w_a, w_x):
    depth = w_a.shape[0]
    hpg = GATE_BLK // LRU_HEAD_DIM
    n_grp = LRU_HEADS // hpg
    eye = jnp.eye(hpg, dtype=w_a.dtype)

    def bd(w):
        w = w.reshape(depth, n_grp, hpg, LRU_HEAD_DIM, LRU_HEAD_DIM)
        full = jnp.einsum('lghde,hk->lghdke', w, eye)
        return full.reshape(depth, n_grp, GATE_BLK, GATE_BLK)

    return jnp.concatenate([bd(w_a), bd(w_x)], axis=-1)


def kernel(x, norm1_g, w_in, lru_conv_w, lru_conv_b, lru_w_a, lru_b_a, lru_w_x, lru_b_x, lru_lambda, sc_conv_w, lru_out_g, sc_out_g, w_out, norm2_g, mlp_w_up, mlp_w_gate, mlp_conv_w, mlp_conv_b, mlp_w_down, final_g):
    bsz, seq, d = x.shape
    depth = w_in.shape[0]
    bf = jnp.bfloat16
    row = lambda a: a.reshape(depth, 1, -1)
    p = {
        "norm1_g": row(norm1_g), "w_in": w_in.astype(bf),
        "lru_conv_w": lru_conv_w, "lru_conv_b": row(lru_conv_b),
        "w_gates": _block_diag_gates(lru_w_a, lru_w_x).astype(bf),
        "lru_b_a": row(lru_b_a), "lru_b_x": row(lru_b_x), "lru_lambda": row(lru_lambda),
        "sc_conv_w": sc_conv_w, "lru_out_g": row(lru_out_g), "sc_out_g": row(sc_out_g),
        "w_out": w_out.astype(bf), "norm2_g": row(norm2_g),
        "mlp_w_up": mlp_w_up.astype(bf), "mlp_w_gate": mlp_w_gate.astype(bf),
        "mlp_conv_w": mlp_conv_w, "mlp_conv_b": row(mlp_conv_b),
        "mlp_w_down": mlp_w_down.astype(bf), "final_g": final_g.reshape(1, -1),
    }
    h = jnp.transpose(x, (1, 0, 2)).reshape(seq * bsz, d)
    for l in range(depth):
        h = _mixer_call(h, l, p, bsz, T_BLK)
        h = _mlp_call(h, l, p, bsz, T_BLK, final_norm=(l == depth - 1))
    return jnp.transpose(h.reshape(seq, bsz, d), (1, 0, 2))
```
